```python
import jax, jax.numpy as jnp
from jax import lax
import numpy as np

D_MODEL = 1024
BATCH = 8
SEQ = 8192
DEPTH = 2

N_MIXERS = 2
HEAD_DIM = 64
N_Q_HEADS = D_MODEL // HEAD_DIM
N_KV_HEADS = max(1, N_Q_HEADS // 8)
Q_PER_KV = N_Q_HEADS // N_KV_HEADS
QKV_DIM = (N_Q_HEADS + 2 * N_KV_HEADS) * HEAD_DIM
WINDOW = 128
BLOCK = 128
ROPE_THETA = 10000.0
CONV_WIDTH = 31
D_FF = -(-(8 * D_MODEL) // (3 * 256)) * 256
N_ATTN_LAYERS = (DEPTH + 1) // 2
N_CONV_LAYERS = DEPTH // 2
RMS_EPS = 1e-5
LN_EPS = 1e-5

kernel_name = "hybrid_swa_sink_conformer_conv"


def rmsnorm(x, g):
    xf = x.astype(jnp.float32)
    y = xf * lax.rsqrt(jnp.mean(xf * xf, axis=-1, keepdims=True) + RMS_EPS)
    return (y * g.astype(jnp.float32)).astype(x.dtype)


def layernorm(x, g, b):
    xf = x.astype(jnp.float32)
    mu = jnp.mean(xf, axis=-1, keepdims=True)
    xc = xf - mu
    var = jnp.mean(xc * xc, axis=-1, keepdims=True)
    y = xc * lax.rsqrt(var + LN_EPS)
    return (y * g.astype(jnp.float32) + b.astype(jnp.float32)).astype(x.dtype)


def rope_tables(seq_len):
    pos = jnp.arange(seq_len, dtype=jnp.float32)
    inv_freq = ROPE_THETA ** (-jnp.arange(0, HEAD_DIM, 2, dtype=jnp.float32) / HEAD_DIM)
    ang = pos[:, None] * inv_freq[None, :]
    return jnp.cos(ang)[:, None, :], jnp.sin(ang)[:, None, :]


def apply_rope(t, cos, sin):
    tf = t.astype(jnp.float32)
    t1, t2 = tf[..., : HEAD_DIM // 2], tf[..., HEAD_DIM // 2:]
    out = jnp.concatenate([t1 * cos - t2 * sin, t2 * cos + t1 * sin], axis=-1)
    return out.astype(t.dtype)


def band_mask(n_blocks):
    i = jnp.arange(BLOCK)[:, None]
    j = jnp.arange(2 * BLOCK)[None, :]
    rel = i + BLOCK - j
    band = (rel >= 0) & (rel < WINDOW)
    k_pos = jnp.arange(n_blocks)[:, None] * BLOCK - BLOCK + jnp.arange(2 * BLOCK)[None, :]
    return band[None, :, :] & (k_pos >= 0)[:, None, :]


def sliding_window_gqa(h, w_qkv, b_qkv, sinks, w_o, b_o, cos, sin):
    B, S, _ = h.shape
    nb = S // BLOCK
    qkv = h @ w_qkv + b_qkv
    q_end = N_Q_HEADS * HEAD_DIM
    k_end = q_end + N_KV_HEADS * HEAD_DIM
    q = qkv[..., :q_end].reshape(B, S, N_Q_HEADS, HEAD_DIM)
    k = qkv[..., q_end:k_end].reshape(B, S, N_KV_HEADS, HEAD_DIM)
    v = qkv[..., k_end:].reshape(B, S, N_KV_HEADS, HEAD_DIM)
    q = apply_rope(q, cos, sin).reshape(B, nb, BLOCK, N_KV_HEADS, Q_PER_KV, HEAD_DIM)
    k = apply_rope(k, cos, sin).reshape(B, nb, BLOCK, N_KV_HEADS, HEAD_DIM)
    v = v.reshape(B, nb, BLOCK, N_KV_HEADS, HEAD_DIM)

    def with_prev(t):
        prev = jnp.pad(t[:, :-1], ((0, 0), (1, 0), (0, 0), (0, 0), (0, 0)))
        return jnp.concatenate([prev, t], axis=2)

    kb, vb = with_prev(k), with_prev(v)
    scores = jnp.einsum('bnqkgd,bnskd->bnkgqs', q, kb,
                        preferred_element_type=jnp.float32) * (HEAD_DIM ** -0.5)
    mask = band_mask(nb)[None, :, None, None]
    scores = jnp.where(mask, scores, jnp.finfo(jnp.float32).min)
    sink = sinks.astype(jnp.float32).reshape(N_KV_HEADS, Q_PER_KV)[None, None, :, :, None, None]
    m = jnp.maximum(jnp.max(scores, axis=-1, keepdims=True), sink)
    p = jnp.exp(scores - m)
    probs = p / (jnp.sum(p, axis=-1, keepdims=True) + jnp.exp(sink - m))
    out = jnp.einsum('bnkgqs,bnskd->bnqkgd', probs.astype(vb.dtype), vb)
    out = out.reshape(B, S, N_Q_HEADS * HEAD_DIM)
    return out @ w_o + b_o


def conformer_conv(h, w_pw1, b_pw1, w_dw, b_dw, ln_g, ln_b, w_pw2, b_pw2):
    a = h @ w_pw1 + b_pw1
    u = a[..., :D_MODEL] * jax.nn.sigmoid(a[..., D_MODEL:])
    u = lax.conv_general_dilated(
        u, w_dw[:, None, :].astype(u.dtype), window_strides=(1,),
        padding=((CONV_WIDTH - 1, 0),),
        dimension_numbers=('NWC', 'WIO', 'NWC'),
        feature_group_count=D_MODEL) + b_dw
    u = jax.nn.silu(layernorm(u, ln_g, ln_b))
    return u @ w_pw2 + b_pw2


def swiglu(h, w1, w3, w2):
    return (jax.nn.silu(h @ w1) * (h @ w3)) @ w2


def setup_inputs(seed: int = 0) -> dict:
    key = jax.random.key(seed)
    ks = jax.random.split(key, 24)
    f32 = jnp.float32
    nrm = lambda k, shape, s: jax.random.normal(k, shape, f32) * s
    D, NA, NC = D_MODEL, N_ATTN_LAYERS, N_CONV_LAYERS
    return {
        "x": nrm(ks[0], (BATCH, SEQ, D), 1.0),
        "norm_mix": 1.0 + nrm(ks[1], (DEPTH, D), 0.02),
        "norm_ffn": 1.0 + nrm(ks[2], (DEPTH, D), 0.02),
        "attn_w_qkv": nrm(ks[3], (NA, D, QKV_DIM), D ** -0.5),
        "attn_b_qkv": nrm(ks[4], (NA, QKV_DIM), 0.02),
        "attn_sinks": nrm(ks[5], (NA, N_Q_HEADS), 0.5),
        "attn_w_o": nrm(ks[6], (NA, N_Q_HEADS * HEAD_DIM, D), (N_Q_HEADS * HEAD_DIM) ** -0.5),
        "attn_b_o": nrm(ks[7], (NA, D), 0.02),
        "conv_w_pw1": nrm(ks[8], (NC, D, 2 * D), D ** -0.5),
        "conv_b_pw1": nrm(ks[9], (NC, 2 * D), 0.02),
        "conv_w_dw": nrm(ks[10], (NC, CONV_WIDTH, D), CONV_WIDTH ** -0.5),
        "conv_b_dw": nrm(ks[11], (NC, D), 0.02),
        "conv_ln_g": 1.0 + nrm(ks[12], (NC, D), 0.02),
        "conv_ln_b": nrm(ks[13], (NC, D), 0.02),
        "conv_w_pw2": nrm(ks[14], (NC, D, D), D ** -0.5),
        "conv_b_pw2": nrm(ks[15], (NC, D), 0.02),
        "ffn_w1": nrm(ks[16], (DEPTH, D, D_FF), D ** -0.5),
        "ffn_w3": nrm(ks[17], (DEPTH, D, D_FF), D ** -0.5),
        "ffn_w2": nrm(ks[18], (DEPTH, D_FF, D), D_FF ** -0.5),
        "norm_final": 1.0 + nrm(ks[19], (D,), 0.02),
    }


def reference(x, norm_mix, norm_ffn, attn_w_qkv, attn_b_qkv, attn_sinks, attn_w_o, attn_b_o,
              conv_w_pw1, conv_b_pw1, conv_w_dw, conv_b_dw, conv_ln_g, conv_ln_b,
              conv_w_pw2, conv_b_pw2, ffn_w1, ffn_w3, ffn_w2, norm_final):
    cos, sin = rope_tables(x.shape[1])
    h = x
    for layer in range(DEPTH):
        y = rmsnorm(h, norm_mix[layer])
        idx = layer // N_MIXERS
        if layer % N_MIXERS == 0:
            y = sliding_window_gqa(y, attn_w_qkv[idx], attn_b_qkv[idx], attn_sinks[idx],
                                   attn_w_o[idx], attn_b_o[idx], cos, sin)
        else:
            y = conformer_conv(y, conv_w_pw1[idx], conv_b_pw1[idx], conv_w_dw[idx],
                               conv_b_dw[idx], conv_ln_g[idx], conv_ln_b[idx],
                               conv_w_pw2[idx], conv_b_pw2[idx])
        h = h + y
        h = h + swiglu(rmsnorm(h, norm_ffn[layer]), ffn_w1[layer], ffn_w3[layer], ffn_w2[layer])
    return rmsnorm(h, norm_final)
```

```python
import functools

import jax
import jax.numpy as jnp
from jax import lax
from jax.experimental import pallas as pl
from jax.experimental.pallas import tpu as pltpu

D_MODEL = 1024
HEAD_DIM = 64
N_Q_HEADS = D_MODEL // HEAD_DIM
N_KV_HEADS = 2
Q_PER_KV = N_Q_HEADS // N_KV_HEADS
KV_DIM = N_KV_HEADS * HEAD_DIM
QKV_DIM = D_MODEL + 2 * KV_DIM
WINDOW = 128
BLOCK = 128
ROPE_THETA = 10000.0
CONV_WIDTH = 31
D_FF = 2816
RMS_EPS = 1e-5
LN_EPS = 1e-5

LANES = 128
PAIRS_PER_KV = Q_PER_KV // 2
CONV_HALO = 32

ATTN_TILE = 512
CONV_TILE = 512
FFN_TILE = 512
FFN_CHUNK = 256
CONV_ROWS = 128
VMEM_LIMIT_BYTES = 56 * 1024 * 1024

F32 = jnp.float32
BF16 = jnp.bfloat16
NEG = float(jnp.finfo(jnp.float32).min)


def _const_spec(shape):
    nd = len(shape)
    return pl.BlockSpec(shape, lambda *_: (0,) * nd, pipeline_mode=pl.Buffered(1))


def _rms(x, g):
    ms = jnp.mean(x * x, axis=-1, keepdims=True)
    return x * lax.rsqrt(ms + RMS_EPS) * g


def _ffn_kernel(*refs, final):
    if final:
        h_ref, g_ref, w1_ref, w3_ref, w2_ref, gf_ref, o_ref, xn_ref, act_ref = refs
    else:
        h_ref, g_ref, w1_ref, w3_ref, w2_ref, o_ref, xn_ref, act_ref = refs
    xn_ref[...] = _rms(h_ref[...], g_ref[...]).astype(BF16)
    for c in range(D_FF // FFN_CHUNK):
        cols = slice(c * FFN_CHUNK, (c + 1) * FFN_CHUNK)
        x = xn_ref[...]
        a = jnp.dot(x, w1_ref[:, cols], preferred_element_type=F32)
        b = jnp.dot(x, w3_ref[:, cols], preferred_element_type=F32)
        act_ref[:, cols] = (a * jax.nn.sigmoid(a) * b).astype(BF16)
    y = jnp.dot(act_ref[...], w2_ref[...], preferred_element_type=F32)
    out = h_ref[...] + y
    if final:
        out = _rms(out, gf_ref[...])
    o_ref[...] = out


def _ffn(h, g, w1, w3, w2, g_final=None):
    t = h.shape[0]
    final = g_final is not None
    tile = pl.BlockSpec((FFN_TILE, D_MODEL), lambda i: (i, 0))
    in_specs = [tile, _const_spec((1, D_MODEL)), _const_spec((D_MODEL, D_FF)),
                _const_spec((D_MODEL, D_FF)), _const_spec((D_FF, D_MODEL))]
    args = [h, g.reshape(1, D_MODEL), w1, w3, w2]
    if final:
        in_specs.append(_const_spec((1, D_MODEL)))
        args.append(g_final.reshape(1, D_MODEL))
    return pl.pallas_call(
        functools.partial(_ffn_kernel, final=final),
        grid=(t // FFN_TILE,),
        in_specs=in_specs,
        out_specs=tile,
        out_shape=jax.ShapeDtypeStruct((t, D_MODEL), F32),
        scratch_shapes=[pltpu.VMEM((FFN_TILE, D_MODEL), BF16), pltpu.VMEM((FFN_TILE, D_FF), BF16)],
        compiler_params=pltpu.CompilerParams(
            dimension_semantics=("arbitrary",), vmem_limit_bytes=VMEM_LIMIT_BYTES),
        name="ffn_final" if final else "ffn",
    )(*args)


def _attn_kernel(sink_ref, x_ref, g_ref, wqkv_ref, bqkv_ref, cos_ref, sin_ref, wo_ref, bo_ref,
                 o_ref, q_ref, k_ref, v_ref, att_ref):
    s = pl.program_id(1)
    nblk = ATTN_TILE // BLOCK
    cur = slice(BLOCK, BLOCK + ATTN_TILE)

    @pl.when(s == 0)
    def _():
        k_ref[:, 0:BLOCK, :] = jnp.zeros((4, BLOCK, LANES), BF16)
        v_ref[:, 0:BLOCK, :] = jnp.zeros((4, BLOCK, LANES), BF16)

    @pl.when(s > 0)
    def _():
        k_ref[:, 0:BLOCK, :] = k_ref[:, ATTN_TILE:ATTN_TILE + BLOCK, :]
        v_ref[:, 0:BLOCK, :] = v_ref[:, ATTN_TILE:ATTN_TILE + BLOCK, :]

    xn = _rms(x_ref[0], g_ref[...]).astype(BF16)
    qkv = jnp.dot(xn, wqkv_ref[...], preferred_element_type=F32) + bqkv_ref[...]

    cos = cos_ref[...]
    sin = sin_ref[...]
    lane = lax.broadcasted_iota(jnp.int32, (ATTN_TILE, LANES), 1)
    first_half = (lane % HEAD_DIM) < (HEAD_DIM // 2)
    lo_lanes = lane < HEAD_DIM

    def rope(t):
        swapped = jnp.where(first_half,
                            pltpu.roll(t, LANES - HEAD_DIM // 2, axis=1),
                            pltpu.roll(t, HEAD_DIM // 2, axis=1))
        return t * cos + swapped * sin

    for c in range(D_MODEL // LANES):
        qc = rope(qkv[:, c * LANES:(c + 1) * LANES]) * (HEAD_DIM ** -0.5)
        q_ref[:, c * LANES:(c + 1) * LANES] = qc.astype(BF16)

    k = rope(qkv[:, D_MODEL:D_MODEL + KV_DIM])
    v = qkv[:, D_MODEL + KV_DIM:QKV_DIM]
    for t, ref in ((k, k_ref), (v, v_ref)):
        t_sw = pltpu.roll(t, HEAD_DIM, axis=1)
        zero = jnp.zeros_like(t)
        ref[0, cur, :] = jnp.where(lo_lanes, t, zero).astype(BF16)
        ref[1, cur, :] = jnp.where(lo_lanes, zero, t_sw).astype(BF16)
        ref[2, cur, :] = jnp.where(lo_lanes, t_sw, zero).astype(BF16)
        ref[3, cur, :] = jnp.where(lo_lanes, zero, t).astype(BF16)

    rows = PAIRS_PER_KV * BLOCK
    qi = lax.broadcasted_iota(jnp.int32, (rows, 2 * BLOCK), 0) % BLOCK
    kj = lax.broadcasted_iota(jnp.int32, (rows, 2 * BLOCK), 1)
    band = (kj > qi) & (kj <= qi + WINDOW)
    pair_of_row = lax.broadcasted_iota(jnp.int32, (rows, 1), 0) // BLOCK

    for j in range(nblk):
        win = slice(j * BLOCK, (j + 2) * BLOCK)
        qrows = slice(j * BLOCK, (j + 1) * BLOCK)
        if j == 0:
            mask = band & (kj >= jnp.where(s == 0, BLOCK, 0))
        else:
            mask = band
        for kh in range(N_KV_HEADS):
            chunks = [kh * PAIRS_PER_KV + p for p in range(PAIRS_PER_KV)]
            qs = jnp.concatenate([q_ref[qrows, c * LANES:(c + 1) * LANES] for c in chunks], axis=0)
            acc = None
            for half in range(2):
                sc = lax.dot_general(qs, k_ref[2 * kh + half, win, :], (((1,), (1,)), ((), ())),
                                     preferred_element_type=F32)
                sc = jnp.where(mask, sc, NEG)
                sink = jnp.full((rows, 1), sink_ref[2 * chunks[-1] + half], F32)
                for p in range(PAIRS_PER_KV - 1):
                    sink = jnp.where(pair_of_row == p, sink_ref[2 * chunks[p] + half], sink)
                m = jnp.maximum(jnp.max(sc, axis=-1, keepdims=True), sink)
                p = jnp.exp(sc - m)
                denom = jnp.sum(p, axis=-1, keepdims=True) + jnp.exp(sink - m)
                probs = (p * (1.0 / denom)).astype(BF16)
                pv = jnp.dot(probs, v_ref[2 * kh + half, win, :], preferred_element_type=F32)
                acc = pv if acc is None else acc + pv
            for p, c in enumerate(chunks):
                att_ref[qrows, c * LANES:(c + 1) * LANES] = acc[p * BLOCK:(p + 1) * BLOCK, :].astype(BF16)

    y = jnp.dot(att_ref[...], wo_ref[...], preferred_element_type=F32) + bo_ref[...]
    o_ref[0] = x_ref[0] + y


def _rope_tables(seq):
    pos = jnp.arange(seq, dtype=F32)
    inv_freq = ROPE_THETA ** (-jnp.arange(0, HEAD_DIM, 2, dtype=F32) / HEAD_DIM)
    ang = pos[:, None] * inv_freq[None, :]
    cos, sin = jnp.cos(ang), jnp.sin(ang)
    reps = LANES // HEAD_DIM
    cos_t = jnp.tile(jnp.concatenate([cos, cos], axis=-1), (1, reps))
    sin_t = jnp.tile(jnp.concatenate([-sin, sin], axis=-1), (1, reps))
    return cos_t, sin_t


def _attention(x, g, w_qkv, b_qkv, sinks, w_o, b_o):
    b, s, _ = x.shape
    cos_t, sin_t = _rope_tables(s)
    tile = pl.BlockSpec((1, ATTN_TILE, D_MODEL), lambda bi, si: (bi, si, 0))
    table = pl.BlockSpec((ATTN_TILE, LANES), lambda bi, si: (si, 0))
    kv_buf = pltpu.VMEM((2 * N_KV_HEADS, BLOCK + ATTN_TILE, LANES), BF16)
    return pl.pallas_call(
        _attn_kernel,
        grid=(b, s // ATTN_TILE),
        in_specs=[pl.BlockSpec(memory_space=pltpu.SMEM), tile, _const_spec((1, D_MODEL)),
                  _const_spec((D_MODEL, QKV_DIM)), _const_spec((1, QKV_DIM)), table, table,
                  _const_spec((D_MODEL, D_MODEL)), _const_spec((1, D_MODEL))],
        out_specs=tile,
        out_shape=jax.ShapeDtypeStruct(x.shape, F32),
        scratch_shapes=[pltpu.VMEM((ATTN_TILE, D_MODEL), BF16), kv_buf, kv_buf,
                        pltpu.VMEM((ATTN_TILE, D_MODEL), BF16)],
        compiler_params=pltpu.CompilerParams(
            dimension_semantics=("arbitrary", "arbitrary"), vmem_limit_bytes=VMEM_LIMIT_BYTES),
        name="swa_attention",
    )(sinks, x, g.reshape(1, D_MODEL), w_qkv, b_qkv.reshape(1, QKV_DIM), cos_t, sin_t,
      w_o, b_o.reshape(1, D_MODEL))


def _conv_kernel(x_ref, g_ref, w1_ref, b1_ref, wdw_ref, bdw_ref, lng_ref, lnb_ref, w2_ref, b2_ref,
                 o_ref, u_ref, c_ref, act_ref):
    s = pl.program_id(1)
    nchunk = D_MODEL // LANES

    @pl.when(s == 0)
    def _():
        u_ref[:, 0:CONV_HALO, :] = jnp.zeros((nchunk, CONV_HALO, LANES), F32)

    @pl.when(s > 0)
    def _():
        u_ref[:, 0:CONV_HALO, :] = u_ref[:, CONV_TILE:CONV_TILE + CONV_HALO, :]

    xn = _rms(x_ref[0], g_ref[...]).astype(BF16)
    a = jnp.dot(xn, w1_ref[...], preferred_element_type=F32) + b1_ref[...]
    for c in range(nchunk):
        lin = a[:, c * LANES:(c + 1) * LANES]
        gate = a[:, D_MODEL + c * LANES:D_MODEL + (c + 1) * LANES]
        u_ref[c, CONV_HALO:CONV_HALO + CONV_TILE, :] = lin * jax.nn.sigmoid(gate)

    base = CONV_HALO - (CONV_WIDTH - 1)

    def conv_chunk(c, carry):
        for r in range(CONV_TILE // CONV_ROWS):
            r0 = r * CONV_ROWS
            acc = jnp.zeros((CONV_ROWS, LANES), F32)
            for k in range(CONV_WIDTH):
                acc = acc + u_ref[c, r0 + base + k:r0 + base + k + CONV_ROWS, :] * wdw_ref[c, k:k + 1, :]
            c_ref[c, r0:r0 + CONV_ROWS, :] = acc + bdw_ref[c]
        return carry

    lax.fori_loop(0, nchunk, conv_chunk, 0)

    tot = c_ref[0]
    for c in range(1, nchunk):
        tot = tot + c_ref[c]
    mu = jnp.sum(tot, axis=-1, keepdims=True) * (1.0 / D_MODEL)
    sq = None
    for c in range(nchunk):
        d = c_ref[c] - mu
        sq = d * d if sq is None else sq + d * d
    var = jnp.sum(sq, axis=-1, keepdims=True) * (1.0 / D_MODEL)
    rstd = lax.rsqrt(var + LN_EPS)
    for c in range(nchunk):
        yv = (c_ref[c] - mu) * rstd * lng_ref[c] + lnb_ref[c]
        act_ref[:, c * LANES:(c + 1) * LANES] = (yv * jax.nn.sigmoid(yv)).astype(BF16)

    y = jnp.dot(act_ref[...], w2_ref[...], preferred_element_type=F32) + b2_ref[...]
    o_ref[0] = x_ref[0] + y


def _chunked(v):
    return v.reshape(D_MODEL // LANES, 1, LANES)


def _conv(x, g, w_pw1, b_pw1, w_dw, b_dw, ln_g, ln_b, w_pw2, b_pw2):
    b, s, _ = x.shape
    nchunk = D_MODEL // LANES
    tile = pl.BlockSpec((1, CONV_TILE, D_MODEL), lambda bi, si: (bi, si, 0))
    wdw = w_dw.reshape(CONV_WIDTH, nchunk, LANES).transpose(1, 0, 2)
    chunk_vec = _const_spec((nchunk, 1, LANES))
    return pl.pallas_call(
        _conv_kernel,
        grid=(b, s // CONV_TILE),
        in_specs=[tile, _const_spec((1, D_MODEL)), _const_spec((D_MODEL, 2 * D_MODEL)),
                  _const_spec((1, 2 * D_MODEL)), _const_spec((nchunk, CONV_WIDTH, LANES)),
                  chunk_vec, chunk_vec, chunk_vec,
                  _const_spec((D_MODEL, D_MODEL)), _const_spec((1, D_MODEL))],
        out_specs=tile,
        out_shape=jax.ShapeDtypeStruct(x.shape, F32),
        scratch_shapes=[pltpu.VMEM((nchunk, CONV_HALO + CONV_TILE, LANES), F32),
                        pltpu.VMEM((nchunk, CONV_TILE, LANES), F32),
                        pltpu.VMEM((CONV_TILE, D_MODEL), BF16)],
        compiler_params=pltpu.CompilerParams(
            dimension_semantics=("arbitrary", "arbitrary"), vmem_limit_bytes=VMEM_LIMIT_BYTES),
        name="conformer_conv",
    )(x, g.reshape(1, D_MODEL), w_pw1, b_pw1.reshape(1, 2 * D_MODEL), wdw, _chunked(b_dw),
      _chunked(ln_g), _chunked(ln_b), w_pw2, b_pw2.reshape(1, D_MODEL))


def kernel(x, norm_mix, norm_ffn, attn_w_qkv, attn_b_qkv, attn_sinks, attn_w_o, attn_b_o, conv_w_pw1, conv_b_pw1, conv_w_dw, conv_b_dw, conv_ln_g, conv_ln_b, conv_w_pw2, conv_b_pw2, ffn_w1, ffn_w3, ffn_w2, norm_final):
    b, s, d = x.shape
    bf = lambda w: w.astype(BF16)
    h = _attention(x, norm_mix[0], bf(attn_w_qkv[0]), attn_b_qkv[0], attn_sinks[0],
                   bf(attn_w_o[0]), attn_b_o[0])
    h = _ffn(h.reshape(b * s, d), norm_ffn[0], bf(ffn_w1[0]), bf(ffn_w3[0]), bf(ffn_w2[0]))
    h = _conv(h.reshape(b, s, d), norm_mix[1], bf(conv_w_pw1[0]), conv_b_pw1[0], conv_w_dw[0],
              conv_b_dw[0], conv_ln_g[0], conv_ln_b[0], bf(conv_w_pw2[0]), conv_b_pw2[0])
    h = _ffn(h.reshape(b * s, d), norm_ffn[1], bf(ffn_w1[1]), bf(ffn_w3[1]), bf(ffn_w2[1]),
             g_final=norm_final)
    return h.reshape(b, s, d)
```

```python
import functools

import jax
import jax.numpy as jnp
from jax import lax
from jax.experimental import pallas as pl
from jax.experimental.pallas import tpu as pltpu

D_MODEL = 1024
HEAD_DIM = 64
N_Q_HEADS = D_MODEL // HEAD_DIM
N_KV_HEADS = 2
Q_PER_KV = N_Q_HEADS // N_KV_HEADS
KV_DIM = N_KV_HEADS * HEAD_DIM
QKV_DIM = D_MODEL + 2 * KV_DIM
WINDOW = 128
BLOCK = 128
ROPE_THETA = 10000.0
CONV_WIDTH = 31
D_FF = 2816
RMS_EPS = 1e-5
LN_EPS = 1e-5

LANES = 128
PAIRS_PER_KV = Q_PER_KV // 2
CONV_HALO = 32

ATTN_TILE = 512
ATTN_BLOCK_UNROLL = 2
CONV_TILE = 512
FFN_TILE = 512
FFN_CHUNK = 256
CONV_ROWS = 128
VMEM_LIMIT_BYTES = 56 * 1024 * 1024

F32 = jnp.float32
BF16 = jnp.bfloat16
NEG = float(jnp.finfo(jnp.float32).min)
LOG2E = 1.4426950408889634
QK_SCALE = HEAD_DIM ** -0.5 * LOG2E


def _const_spec(shape):
    nd = len(shape)
    return pl.BlockSpec(shape, lambda *_: (0,) * nd, pipeline_mode=pl.Buffered(1))


def _rms(x, g):
    ms = jnp.mean(x * x, axis=-1, keepdims=True)
    return x * lax.rsqrt(ms + RMS_EPS) * g


def _ffn_kernel(*refs, final):
    if final:
        h_ref, g_ref, w1_ref, w3_ref, w2_ref, gf_ref, o_ref, xn_ref, act_ref = refs
    else:
        h_ref, g_ref, w1_ref, w3_ref, w2_ref, o_ref, xn_ref, act_ref = refs
    xn_ref[...] = _rms(h_ref[...], g_ref[...]).astype(BF16)
    for c in range(D_FF // FFN_CHUNK):
        cols = slice(c * FFN_CHUNK, (c + 1) * FFN_CHUNK)
        x = xn_ref[...]
        a = jnp.dot(x, w1_ref[:, cols], preferred_element_type=F32)
        b = jnp.dot(x, w3_ref[:, cols], preferred_element_type=F32)
        act_ref[:, cols] = (a * jax.nn.sigmoid(a) * b).astype(BF16)
    y = jnp.dot(act_ref[...], w2_ref[...], preferred_element_type=F32)
    out = h_ref[...] + y
    if final:
        out = _rms(out, gf_ref[...])
    o_ref[...] = out


def _ffn(h, g, w1, w3, w2, g_final=None):
    t = h.shape[0]
    final = g_final is not None
    tile = pl.BlockSpec((FFN_TILE, D_MODEL), lambda i: (i, 0))
    in_specs = [tile, _const_spec((1, D_MODEL)), _const_spec((D_MODEL, D_FF)),
                _const_spec((D_MODEL, D_FF)), _const_spec((D_FF, D_MODEL))]
    args = [h, g.reshape(1, D_MODEL), w1, w3, w2]
    if final:
        in_specs.append(_const_spec((1, D_MODEL)))
        args.append(g_final.reshape(1, D_MODEL))
    return pl.pallas_call(
        functools.partial(_ffn_kernel, final=final),
        grid=(t // FFN_TILE,),
        in_specs=in_specs,
        out_specs=tile,
        out_shape=jax.ShapeDtypeStruct((t, D_MODEL), F32),
        scratch_shapes=[pltpu.VMEM((FFN_TILE, D_MODEL), BF16), pltpu.VMEM((FFN_TILE, D_FF), BF16)],
        compiler_params=pltpu.CompilerParams(
            dimension_semantics=("arbitrary",), vmem_limit_bytes=VMEM_LIMIT_BYTES),
        name="ffn_final" if final else "ffn",
    )(*args)


def _attn_kernel(sink_ref, x_ref, g_ref, wqkv_ref, bqkv_ref, cos_ref, sin_ref, wo_ref, bo_ref,
                 o_ref, q_ref, k_ref, v_ref, att_ref):
    s = pl.program_id(1)
    nblk = ATTN_TILE // BLOCK
    cur = slice(BLOCK, BLOCK + ATTN_TILE)

    @pl.when(s == 0)
    def _():
        k_ref[:, 0:BLOCK, :] = jnp.zeros((4, BLOCK, LANES), BF16)
        v_ref[:, 0:BLOCK, :] = jnp.zeros((4, BLOCK, 2 * LANES), BF16)

    @pl.when(s > 0)
    def _():
        k_ref[:, 0:BLOCK, :] = k_ref[:, ATTN_TILE:ATTN_TILE + BLOCK, :]
        v_ref[:, 0:BLOCK, :] = v_ref[:, ATTN_TILE:ATTN_TILE + BLOCK, :]

    xn = _rms(x_ref[0], g_ref[...]).astype(BF16)
    qkv = jnp.dot(xn, wqkv_ref[...], preferred_element_type=F32) + bqkv_ref[...]

    cos = cos_ref[...]
    sin = sin_ref[...]
    lane = lax.broadcasted_iota(jnp.int32, (ATTN_TILE, LANES), 1)
    first_half = (lane % HEAD_DIM) < (HEAD_DIM // 2)
    lo_lanes = lane < HEAD_DIM

    def rope(t, cos_t, sin_t):
        swapped = jnp.where(first_half,
                            pltpu.roll(t, LANES - HEAD_DIM // 2, axis=1),
                            pltpu.roll(t, HEAD_DIM // 2, axis=1))
        return t * cos_t + swapped * sin_t

    cos_q, sin_q = cos * QK_SCALE, sin * QK_SCALE
    for c in range(D_MODEL // LANES):
        qc = rope(qkv[:, c * LANES:(c + 1) * LANES], cos_q, sin_q)
        q_ref[:, c * LANES:(c + 1) * LANES] = qc.astype(BF16)

    k = rope(qkv[:, D_MODEL:D_MODEL + KV_DIM], cos, sin)
    v = qkv[:, D_MODEL + KV_DIM:QKV_DIM]
    zero = jnp.zeros((ATTN_TILE, LANES), F32)
    one = jnp.ones((ATTN_TILE, LANES), F32)
    for t, ref in ((k, k_ref), (v, v_ref)):
        t_sw = pltpu.roll(t, HEAD_DIM, axis=1)
        ref[0, cur, 0:LANES] = jnp.where(lo_lanes, t, zero).astype(BF16)
        ref[1, cur, 0:LANES] = jnp.where(lo_lanes, zero, t_sw).astype(BF16)
        ref[2, cur, 0:LANES] = jnp.where(lo_lanes, t_sw, zero).astype(BF16)
        ref[3, cur, 0:LANES] = jnp.where(lo_lanes, zero, t).astype(BF16)
    for idx in range(2 * N_KV_HEADS):
        ones_half = jnp.where(lo_lanes, one, zero) if idx % 2 == 0 else jnp.where(lo_lanes, zero, one)
        v_ref[idx, cur, LANES:2 * LANES] = ones_half.astype(BF16)

    rows = PAIRS_PER_KV * BLOCK

    def block_body(j, carry):
        qi = lax.broadcasted_iota(jnp.int32, (rows, BLOCK), 0) % BLOCK
        kj = lax.broadcasted_iota(jnp.int32, (rows, BLOCK), 1)
        use_prev = kj > qi
        pair_of_row = lax.broadcasted_iota(jnp.int32, (rows, 1), 0) // BLOCK
        lo_out = lax.broadcasted_iota(jnp.int32, (rows, LANES), 1) < HEAD_DIM
        pzero = jnp.zeros((rows, BLOCK), F32)
        r0 = pl.multiple_of(j * BLOCK, BLOCK)
        win = pl.ds(r0, 2 * BLOCK)
        qrows = pl.ds(r0, BLOCK)
        no_prev = jnp.logical_and(s == 0, j == 0)
        for kh in range(N_KV_HEADS):
            chunks = [kh * PAIRS_PER_KV + p for p in range(PAIRS_PER_KV)]
            qs = jnp.concatenate([q_ref[qrows, c * LANES:(c + 1) * LANES] for c in chunks], axis=0)
            acc = None
            sink_terms = []
            for half in range(2):
                sc = lax.dot_general(qs, k_ref[2 * kh + half, win, :], (((1,), (1,)), ((), ())),
                                     preferred_element_type=F32)
                prev = jnp.where(no_prev, NEG, sc[:, 0:BLOCK])
                sf = jnp.where(use_prev, prev, sc[:, BLOCK:2 * BLOCK])
                sink = jnp.full((rows, 1), sink_ref[2 * chunks[-1] + half] * LOG2E, F32)
                for p in range(PAIRS_PER_KV - 1):
                    sink = jnp.where(pair_of_row == p, sink_ref[2 * chunks[p] + half] * LOG2E, sink)
                m = jnp.maximum(jnp.max(sf, axis=-1, keepdims=True), sink)
                p = jnp.exp2(sf - m)
                pcat = jnp.concatenate([jnp.where(use_prev, p, pzero), jnp.where(use_prev, pzero, p)],
                                       axis=1).astype(BF16)
                pv = jnp.dot(pcat, v_ref[2 * kh + half, win, :], preferred_element_type=F32)
                acc = pv if acc is None else acc + pv
                sink_terms.append(jnp.exp2(sink - m))
            denom = acc[:, LANES:2 * LANES] + jnp.where(lo_out, sink_terms[0], sink_terms[1])
            att = acc[:, 0:LANES] * (1.0 / denom)
            for p, c in enumerate(chunks):
                att_ref[qrows, c * LANES:(c + 1) * LANES] = att[p * BLOCK:(p + 1) * BLOCK, :].astype(BF16)
        return carry

    lax.fori_loop(0, nblk, block_body, 0, unroll=ATTN_BLOCK_UNROLL)

    y = jnp.dot(att_ref[...], wo_ref[...], preferred_element_type=F32) + bo_ref[...]
    o_ref[0] = x_ref[0] + y


def _rope_tables(seq):
    pos = jnp.arange(seq, dtype=F32)
    inv_freq = ROPE_THETA ** (-jnp.arange(0, HEAD_DIM, 2, dtype=F32) / HEAD_DIM)
    ang = pos[:, None] * inv_freq[None, :]
    cos, sin = jnp.cos(ang), jnp.sin(ang)
    reps = LANES // HEAD_DIM
    cos_t = jnp.tile(jnp.concatenate([cos, cos], axis=-1), (1, reps))
    sin_t = jnp.tile(jnp.concatenate([-sin, sin], axis=-1), (1, reps))
    return cos_t, sin_t


def _attention(x, g, w_qkv, b_qkv, sinks, w_o, b_o):
    b, s, _ = x.shape
    cos_t, sin_t = _rope_tables(s)
    tile = pl.BlockSpec((1, ATTN_TILE, D_MODEL), lambda bi, si: (bi, si, 0))
    table = pl.BlockSpec((ATTN_TILE, LANES), lambda bi, si: (si, 0))
    k_buf = pltpu.VMEM((2 * N_KV_HEADS, BLOCK + ATTN_TILE, LANES), BF16)
    v_buf = pltpu.VMEM((2 * N_KV_HEADS, BLOCK + ATTN_TILE, 2 * LANES), BF16)
    return pl.pallas_call(
        _attn_kernel,
        grid=(b, s // ATTN_TILE),
        in_specs=[pl.BlockSpec(memory_space=pltpu.SMEM), tile, _const_spec((1, D_MODEL)),
                  _const_spec((D_MODEL, QKV_DIM)), _const_spec((1, QKV_DIM)), table, table,
                  _const_spec((D_MODEL, D_MODEL)), _const_spec((1, D_MODEL))],
        out_specs=tile,
        out_shape=jax.ShapeDtypeStruct(x.shape, F32),
        scratch_shapes=[pltpu.VMEM((ATTN_TILE, D_MODEL), BF16), k_buf, v_buf,
                        pltpu.VMEM((ATTN_TILE, D_MODEL), BF16)],
        compiler_params=pltpu.CompilerParams(
            dimension_semantics=("arbitrary", "arbitrary"), vmem_limit_bytes=VMEM_LIMIT_BYTES),
        name="swa_attention",
    )(sinks, x, g.reshape(1, D_MODEL), w_qkv, b_qkv.reshape(1, QKV_DIM), cos_t, sin_t,
      w_o, b_o.reshape(1, D_MODEL))


def _conv_kernel(x_ref, g_ref, w1_ref, b1_ref, wdw_ref, bdw_ref, lng_ref, lnb_ref, w2_ref, b2_ref,
                 o_ref, u_ref, c_ref, act_ref):
    s = pl.program_id(1)
    nchunk = D_MODEL // LANES

    @pl.when(s == 0)
    def _():
        u_ref[:, 0:CONV_HALO, :] = jnp.zeros((nchunk, CONV_HALO, LANES), F32)

    @pl.when(s > 0)
    def _():
        u_ref[:, 0:CONV_HALO, :] = u_ref[:, CONV_TILE:CONV_TILE + CONV_HALO, :]

    xn = _rms(x_ref[0], g_ref[...]).astype(BF16)
    a = jnp.dot(xn, w1_ref[...], preferred_element_type=F32) + b1_ref[...]
    for c in range(nchunk):
        lin = a[:, c * LANES:(c + 1) * LANES]
        gate = a[:, D_MODEL + c * LANES:D_MODEL + (c + 1) * LANES]
        u_ref[c, CONV_HALO:CONV_HALO + CONV_TILE, :] = lin * jax.nn.sigmoid(gate)

    base = CONV_HALO - (CONV_WIDTH - 1)

    def conv_chunk(c, carry):
        for r in range(CONV_TILE // CONV_ROWS):
            r0 = r * CONV_ROWS
            acc = jnp.zeros((CONV_ROWS, LANES), F32)
            for k in range(CONV_WIDTH):
                acc = acc + u_ref[c, r0 + base + k:r0 + base + k + CONV_ROWS, :] * wdw_ref[c, k:k + 1, :]
            c_ref[c, r0:r0 + CONV_ROWS, :] = acc + bdw_ref[c]
        return carry

    lax.fori_loop(0, nchunk, conv_chunk, 0)

    tot = c_ref[0]
    for c in range(1, nchunk):
        tot = tot + c_ref[c]
    mu = jnp.sum(tot, axis=-1, keepdims=True) * (1.0 / D_MODEL)
    sq = None
    for c in range(nchunk):
        d = c_ref[c] - mu
        sq = d * d if sq is None else sq + d * d
    var = jnp.sum(sq, axis=-1, keepdims=True) * (1.0 / D_MODEL)
    rstd = lax.rsqrt(var + LN_EPS)
    for c in range(nchunk):
        yv = (c_ref[c] - mu) * rstd * lng_ref[c] + lnb_ref[c]
        act_ref[:, c * LANES:(c + 1) * LANES] = (yv * jax.nn.sigmoid(yv)).astype(BF16)

    y = jnp.dot(act_ref[...], w2_ref[...], preferred_element_type=F32) + b2_ref[...]
    o_ref[0] = x_ref[0] + y


def _chunked(v):
    return v.reshape(D_MODEL // LANES, 1, LANES)


def _conv(x, g, w_pw1, b_pw1, w_dw, b_dw, ln_g, ln_b, w_pw2, b_pw2):
    b, s, _ = x.shape
    nchunk = D_MODEL // LANES
    tile = pl.BlockSpec((1, CONV_TILE, D_MODEL), lambda bi, si: (bi, si, 0))
    wdw = w_dw.reshape(CONV_WIDTH, nchunk, LANES).transpose(1, 0, 2)
    chunk_vec = _const_spec((nchunk, 1, LANES))
    return pl.pallas_call(
        _conv_kernel,
        grid=(b, s // CONV_TILE),
        in_specs=[tile, _const_spec((1, D_MODEL)), _const_spec((D_MODEL, 2 * D_MODEL)),
                  _const_spec((1, 2 * D_MODEL)), _const_spec((nchunk, CONV_WIDTH, LANES)),
                  chunk_vec, chunk_vec, chunk_vec,
                  _const_spec((D_MODEL, D_MODEL)), _const_spec((1, D_MODEL))],
        out_specs=tile,
        out_shape=jax.ShapeDtypeStruct(x.shape, F32),
        scratch_shapes=[pltpu.VMEM((nchunk, CONV_HALO + CONV_TILE, LANES), F32),
                        pltpu.VMEM((nchunk, CONV_TILE, LANES), F32),
                        pltpu.VMEM((CONV_TILE, D_MODEL), BF16)],
        compiler_params=pltpu.CompilerParams(
            dimension_semantics=("arbitrary", "arbitrary"), vmem_limit_bytes=VMEM_LIMIT_BYTES),
        name="conformer_conv",
    )(x, g.reshape(1, D_MODEL), w_pw1, b_pw1.reshape(1, 2 * D_MODEL), wdw, _chunked(b_dw),
      _chunked(ln_g), _chunked(ln_b), w_pw2, b_pw2.reshape(1, D_MODEL))


def kernel(x, norm_mix, norm_ffn, attn_w_qkv, attn_b_qkv, attn_sinks, attn_w_o, attn_b_o, conv_w_pw1, conv_b_pw1, conv_w_dw, conv_b_dw, conv_ln_g, conv_ln_b, conv_w_pw2, conv_b_pw2, ffn_w1, ffn_w3, ffn_w2, norm_final):
    b, s, d = x.shape
    bf = lambda w: w.astype(BF16)
    h = _attention(x, norm_mix[0], bf(attn_w_qkv[0]), attn_b_qkv[0], attn_sinks[0],
                   bf(attn_w_o[0]), attn_b_o[0])
    h = _ffn(h.reshape(b * s, d), norm_ffn[0], bf(ffn_w1[0]), bf(ffn_w3[0]), bf(ffn_w2[0]))
    h = _conv(h.reshape(b, s, d), norm_mix[1], bf(conv_w_pw1[0]), conv_b_pw1[0], conv_w_dw[0],
              conv_b_dw[0], conv_ln_g[0], conv_ln_b[0], bf(conv_w_pw2[0]), conv_b_pw2[0])
    h = _ffn(h.reshape(b * s, d), norm_ffn[1], bf(ffn_w1[1]), bf(ffn_w3[1]), bf(ffn_w2[1]),
             g_final=norm_final)
    return h.reshape(b, s, d)
```
